```python
import math
import jax, jax.numpy as jnp
from jax import lax
import numpy as np

D_MODEL = 4096
BATCH = 1
SEQ = 16384
DEPTH = 1
DEC_BATCH = 32
DEC_SEQ = 32
PAST_LEN = 4096

CHUNK = 64
Q_BLOCK = 128
NH_A = 8
DK_A = D_MODEL // 16
DV_A = D_MODEL // 8
NH_B = 16
KVH_B = 4
HD_B = 128
NI_H = 16
NI_D = 64
TOPK_MAX = 256
N_BUCKETS = 32
MAX_DIST = 128
N_GROUPS = 4
EXPERTS_PER_GROUP = 8
N_EXPERTS = N_GROUPS * EXPERTS_PER_GROUP
D_EXPERT = D_MODEL // 4
EPS = 1e-6
SPLIT_SIZES = (NH_A * DK_A, NH_A * DK_A, NH_A * DV_A, NH_A * DV_A, NH_A, NH_A,
               NH_B * HD_B, KVH_B * HD_B, KVH_B * HD_B, NI_H * NI_D, NI_D, NI_H,
               D_MODEL, D_MODEL)
N_IN = sum(SPLIT_SIZES)

kernel_name = 'mlstm_dsa_hmoe_streaming_step'


def rmsnorm(x, g):
    xf = x.astype(jnp.float32)
    y = xf * lax.rsqrt(jnp.mean(xf * xf, -1, keepdims=True) + EPS)
    return (y * g.astype(jnp.float32)).astype(x.dtype)


def split_proj(p):
    points = np.cumsum(np.array(SPLIT_SIZES))[:-1].tolist()
    return jnp.split(p, points, axis=-1)


def mlstm_block(state, inp):
    C0, n0, m0 = state
    q, k, v, ig, lf = inp
    L = q.shape[1]
    b = jnp.cumsum(lf, axis=1).transpose(0, 2, 1)
    igt = ig.transpose(0, 2, 1)
    causal = jnp.tril(jnp.ones((L, L), dtype=bool))
    logw = jnp.where(causal, b[..., :, None] - b[..., None, :] + igt[..., None, :], -jnp.inf)
    g = b + m0[..., None]
    m = jnp.maximum(g, jnp.max(logw, -1))
    w = jnp.exp(logw - m[..., None])
    inter = jnp.exp(g - m)
    s = w * jnp.einsum('bthd,bshd->bhts', q, k)
    num = jnp.einsum('bhts,bshv->bthv', s, v) + jnp.einsum('bht,bhvd,bthd->bthv', inter, C0, q)
    den = jnp.sum(s, -1) + inter * jnp.einsum('bhd,bthd->bht', n0, q)
    h = num / jnp.maximum(jnp.abs(den), jnp.exp(-m)).transpose(0, 2, 1)[..., None]
    m_last = m[..., -1]
    w_last = w[..., -1, :]
    decay = inter[..., -1]
    C1 = decay[..., None, None] * C0 + jnp.einsum('bhs,bshv,bshd->bhvd', w_last, v, k)
    n1 = decay[..., None] * n0 + jnp.einsum('bhs,bshd->bhd', w_last, k)
    return (C1, n1, m_last), h


def mlstm_scan(q, k, v, ig, lf, state):
    B, T = q.shape[:2]
    L = min(CHUNK, T)
    nb = T // L

    def blk(a):
        return jnp.moveaxis(a.reshape((B, nb, L) + a.shape[2:]), 1, 0)

    state, h = lax.scan(mlstm_block, state, (blk(q), blk(k), blk(v), blk(ig), blk(lf)))
    return jnp.moveaxis(h, 0, 1).reshape(B, T, NH_A, DV_A), state


def t5_bucket(rel):
    half = N_BUCKETS // 2
    max_exact = half // 2
    n = jnp.abs(rel)
    nf = jnp.maximum(n, 1).astype(jnp.float32)
    large = max_exact + (jnp.log(nf / max_exact) / math.log(MAX_DIST / max_exact)
                         * (half - max_exact)).astype(jnp.int32)
    large = jnp.minimum(large, half - 1)
    return (rel > 0).astype(jnp.int32) * half + jnp.where(n < max_exact, n, large)


def dsa_attend(q, qi, wi, q_pos, k_all, v_all, ki_all, k_pos, rel_bias, topk):
    f32 = jnp.float32
    B, Tq = q.shape[:2]
    G = NH_B // KVH_B
    dots = jnp.einsum('bthd,bsd->bths', qi.astype(f32), ki_all.astype(f32))
    score = jnp.einsum('bth,bths->bts', wi.astype(f32), jax.nn.relu(dots))
    allowed = (k_pos[None, :] // CHUNK) <= (q_pos[:, None] // CHUNK)
    score = jnp.where(allowed[None], score, -jnp.inf)
    top_val, idx = lax.top_k(score, topk)
    valid = jnp.isfinite(top_val)
    gather = jax.vmap(lambda a, i: a[i])
    ks = gather(k_all, idx)
    vs = gather(v_all, idx)
    bias = rel_bias[t5_bucket(k_pos[idx] - q_pos[None, :, None])]
    qg = q.reshape(B, Tq, KVH_B, G, HD_B)
    logits = jnp.einsum('btkgd,btskd->btkgs', qg, ks).astype(f32) * HD_B ** -0.5
    logits = logits.reshape(B, Tq, NH_B, topk) + jnp.swapaxes(bias, 2, 3).astype(f32)
    logits = jnp.where(valid[:, :, None, :], logits, -jnp.inf)
    p = jax.nn.softmax(logits, -1).astype(v_all.dtype)
    out = jnp.einsum('btkgs,btskd->btkgd', p.reshape(B, Tq, KVH_B, G, topk), vs)
    return out.reshape(B, Tq, NH_B * HD_B)


def token_mixers(h, mstate, past, w_in, b_igate, b_fgate, g_mhnorm, w_a_up, w_b_up, w_out, rel_bias):
    f32 = jnp.float32
    B, T, _ = h.shape
    qa, ka, va, oa, ia, fa, qb, kb, vb, qi, ki, wi, ga, gb = split_proj(h @ w_in)
    mq = qa.reshape(B, T, NH_A, DK_A).astype(f32)
    mk = ka.reshape(B, T, NH_A, DK_A).astype(f32) * DK_A ** -0.5
    mv = va.reshape(B, T, NH_A, DV_A).astype(f32)
    ig = ia.astype(f32) + b_igate.astype(f32)
    lf = jax.nn.log_sigmoid(fa.astype(f32) + b_fgate.astype(f32))
    hm, new_state = mlstm_scan(mq, mk, mv, ig, lf, mstate)
    hm = hm * lax.rsqrt(jnp.mean(hm * hm, -1, keepdims=True) + EPS) * g_mhnorm.astype(f32).reshape(NH_A, DV_A)
    ya = (jax.nn.sigmoid(oa.astype(f32)) * hm.reshape(B, T, NH_A * DV_A)).astype(h.dtype) @ w_a_up
    q = qb.reshape(B, T, NH_B, HD_B)
    k = kb.reshape(B, T, KVH_B, HD_B)
    v = vb.reshape(B, T, KVH_B, HD_B)
    qi = qi.reshape(B, T, NI_H, NI_D) * NI_D ** -0.5
    wi = wi * NI_H ** -0.5
    if past is None:
        k_all, v_all, ki_all = k, v, ki
    else:
        k_all = jnp.concatenate([past[0], k], axis=1)
        v_all = jnp.concatenate([past[1], v], axis=1)
        ki_all = jnp.concatenate([past[2], ki], axis=1)
    L = k_all.shape[1]
    topk = min(TOPK_MAX, L // 4)
    k_pos = jnp.arange(L, dtype=jnp.int32)
    q_pos = (L - T) + jnp.arange(T, dtype=jnp.int32)
    if T % Q_BLOCK == 0:
        nqb = T // Q_BLOCK

        def qblk(a):
            return jnp.moveaxis(a.reshape((B, nqb, Q_BLOCK) + a.shape[2:]), 1, 0)

        att = lax.map(lambda a: dsa_attend(a[0], a[1], a[2], a[3], k_all, v_all, ki_all, k_pos, rel_bias, topk),
                      (qblk(q), qblk(qi), qblk(wi), q_pos.reshape(nqb, Q_BLOCK)))
        att = jnp.moveaxis(att, 0, 1).reshape(B, T, NH_B * HD_B)
    else:
        att = dsa_attend(q, qi, wi, q_pos, k_all, v_all, ki_all, k_pos, rel_bias, topk)
    yb = att @ w_b_up
    merged = (jax.nn.sigmoid(ga.astype(f32)) * ya.astype(f32)
              + jax.nn.sigmoid(gb.astype(f32)) * yb.astype(f32))
    y = merged.astype(h.dtype) @ w_out
    return y, (k, v, ki), new_state


def hier_moe(h, w_rgrp, b_rgrp, w_rexp, b_rexp, w_e_gate, w_e_up, w_e_down):
    f32 = jnp.float32
    B, T, D = h.shape
    hf = h.reshape(B * T, D)
    glog = (hf @ w_rgrp).astype(f32) + b_rgrp.astype(f32)
    gprob = jax.nn.softmax(glog, -1)
    p_grp, grp = lax.top_k(gprob, 1)
    elog = ((hf @ w_rexp).astype(f32) + b_rexp.astype(f32)).reshape(-1, N_GROUPS, EXPERTS_PER_GROUP)
    elog_g = jnp.take_along_axis(elog, grp[:, :, None], axis=1)[:, 0]
    top_l, top_i = lax.top_k(elog_g, 2)
    top_p = jax.nn.softmax(top_l, -1) * p_grp
    eid = grp * EXPERTS_PER_GROUP + top_i
    gate = jnp.einsum('nke,nk->ne', jax.nn.one_hot(eid, N_EXPERTS, dtype=f32), top_p)
    out = jnp.zeros((B * T, D), f32)
    for e in range(N_EXPERTS):
        act = jax.nn.silu(hf @ w_e_gate[e]) * (hf @ w_e_up[e])
        out = out + gate[:, e:e + 1] * (act @ w_e_down[e]).astype(f32)
    return out.astype(h.dtype).reshape(B, T, D)


def setup_inputs(seed: int = 0) -> dict:
    key = jax.random.key(seed)
    ks = jax.random.split(key, 32)
    f32 = jnp.float32

    def nrm(k, shape, scale):
        return jax.random.normal(k, shape, f32) * scale

    return {
        'x_prompt': nrm(ks[0], (BATCH, SEQ, D_MODEL), 1.0),
        'x_sample': nrm(ks[1], (DEC_BATCH, DEC_SEQ, D_MODEL), 1.0),
        'cache_k': nrm(ks[2], (DEPTH, DEC_BATCH, PAST_LEN, KVH_B, HD_B), 1.0),
        'cache_v': nrm(ks[3], (DEPTH, DEC_BATCH, PAST_LEN, KVH_B, HD_B), 1.0),
        'cache_kidx': nrm(ks[4], (DEPTH, DEC_BATCH, PAST_LEN, NI_D), 1.0),
        'state_C': nrm(ks[5], (DEPTH, DEC_BATCH, NH_A, DV_A, DK_A), 0.1),
        'state_n': nrm(ks[6], (DEPTH, DEC_BATCH, NH_A, DK_A), 0.1),
        'state_m': nrm(ks[7], (DEPTH, DEC_BATCH, NH_A), 0.5),
        'g_mix': 1.0 + nrm(ks[8], (DEPTH, D_MODEL), 0.1),
        'w_in': nrm(ks[9], (DEPTH, D_MODEL, N_IN), D_MODEL ** -0.5),
        'b_igate': -2.0 + nrm(ks[10], (DEPTH, NH_A), 0.1),
        'b_fgate': 3.0 + nrm(ks[11], (DEPTH, NH_A), 0.1),
        'g_mhnorm': 1.0 + nrm(ks[12], (DEPTH, NH_A * DV_A), 0.1),
        'w_a_up': nrm(ks[13], (DEPTH, NH_A * DV_A, D_MODEL), (NH_A * DV_A) ** -0.5),
        'w_b_up': nrm(ks[14], (DEPTH, NH_B * HD_B, D_MODEL), (NH_B * HD_B) ** -0.5),
        'w_out': nrm(ks[15], (DEPTH, D_MODEL, D_MODEL), D_MODEL ** -0.5),
        'rel_bias': nrm(ks[16], (N_BUCKETS, NH_B), 0.5),
        'g_ffn': 1.0 + nrm(ks[17], (DEPTH, D_MODEL), 0.1),
        'w_rgrp': nrm(ks[18], (DEPTH, D_MODEL, N_GROUPS), D_MODEL ** -0.5),
        'b_rgrp': nrm(ks[19], (DEPTH, N_GROUPS), 0.01),
        'w_rexp': nrm(ks[20], (DEPTH, D_MODEL, N_EXPERTS), D_MODEL ** -0.5),
        'b_rexp': nrm(ks[21], (DEPTH, N_EXPERTS), 0.01),
        'w_e_gate': nrm(ks[22], (DEPTH, N_EXPERTS, D_MODEL, D_EXPERT), D_MODEL ** -0.5),
        'w_e_up': nrm(ks[23], (DEPTH, N_EXPERTS, D_MODEL, D_EXPERT), D_MODEL ** -0.5),
        'w_e_down': nrm(ks[24], (DEPTH, N_EXPERTS, D_EXPERT, D_MODEL), D_EXPERT ** -0.5),
        'g_final': 1.0 + nrm(ks[25], (D_MODEL,), 0.1),
    }


def _stack(lst, i):
    return jnp.stack([e[i] for e in lst], axis=0)


def reference(x_prompt, x_sample, cache_k, cache_v, cache_kidx, state_C, state_n, state_m,
              g_mix, w_in, b_igate, b_fgate, g_mhnorm, w_a_up, w_b_up, w_out, rel_bias,
              g_ffn, w_rgrp, b_rgrp, w_rexp, b_rexp, w_e_gate, w_e_up, w_e_down, g_final):
    f32 = jnp.float32
    xp, xs = x_prompt, x_sample
    new_p, new_s = [], []
    for l in range(DEPTH):
        def layer(x, mstate, past):
            h = rmsnorm(x, g_mix[l])
            y, rows, st = token_mixers(h, mstate, past, w_in[l], b_igate[l], b_fgate[l], g_mhnorm[l],
                                       w_a_up[l], w_b_up[l], w_out[l], rel_bias)
            x = x + y
            x = x + hier_moe(rmsnorm(x, g_ffn[l]), w_rgrp[l], b_rgrp[l], w_rexp[l], b_rexp[l],
                             w_e_gate[l], w_e_up[l], w_e_down[l])
            return x, rows + st

        bp = xp.shape[0]
        init = (jnp.zeros((bp, NH_A, DV_A, DK_A), f32), jnp.zeros((bp, NH_A, DK_A), f32),
                jnp.zeros((bp, NH_A), f32))
        xp, sp = layer(xp, init, None)
        xs, ss = layer(xs, (state_C[l].astype(f32), state_n[l].astype(f32), state_m[l].astype(f32)),
                       (cache_k[l], cache_v[l], cache_kidx[l]))
        new_p.append(sp)
        new_s.append(ss)
    y_prompt = rmsnorm(xp, g_final)
    y_sample = rmsnorm(xs, g_final)
    return (y_prompt, y_sample,
            _stack(new_p, 0), _stack(new_p, 1), _stack(new_p, 2),
            _stack(new_p, 3), _stack(new_p, 4), _stack(new_p, 5),
            _stack(new_s, 0), _stack(new_s, 1), _stack(new_s, 2),
            _stack(new_s, 3), _stack(new_s, 4), _stack(new_s, 5))
```

```python
import functools

import jax
import jax.numpy as jnp
from jax import lax
from jax.experimental import pallas as pl
from jax.experimental.pallas import tpu as pltpu

F32 = jnp.float32
BF16 = jnp.bfloat16
I32 = jnp.int32

D_MODEL = 4096
CHUNK = 64
NH_A, DK_A, DV_A = 8, 256, 512
NH_B, KVH_B, HD_B = 16, 4, 128
NI_H, NI_D = 16, 64
TOPK_MAX = 256
N_BUCKETS = 32
N_GROUPS, EXPERTS_PER_GROUP = 4, 8
N_EXPERTS = N_GROUPS * EXPERTS_PER_GROUP
D_EXPERT = 1024
EPS = 1e-6

LANE = 128
VMEM_LIMIT = 56 * 1024 * 1024
NEG_BIG = -1e30
INT_MIN = -(2 ** 31)
T5_FAR = 91
T5_LOG_STARTS = (12, 16, 23, 32, 46, 64, 91)
MLSTM_BLOCK = 256


def _params(sem):
    return pltpu.CompilerParams(dimension_semantics=sem, vmem_limit_bytes=VMEM_LIMIT)


def _sigmoid(x):
    return 1.0 / (1.0 + jnp.exp(-x))


def _log_sigmoid(x):
    return jnp.minimum(x, 0.0) - jnp.log(1.0 + jnp.exp(-jnp.abs(x)))


def _rmsnorm_kernel(x_ref, g_ref, o_ref):
    x = x_ref[...]
    ms = jnp.mean(x * x, axis=-1, keepdims=True)
    o_ref[...] = (x * lax.rsqrt(ms + EPS) * g_ref[...]).astype(o_ref.dtype)


def _rmsnorm(x, g, out_dtype, tm=256):
    n, d = x.shape
    return pl.pallas_call(
        _rmsnorm_kernel,
        grid=(n // tm,),
        in_specs=[pl.BlockSpec((tm, d), lambda i: (i, 0)),
                  pl.BlockSpec((1, d), lambda i: (0, 0))],
        out_specs=pl.BlockSpec((tm, d), lambda i: (i, 0)),
        out_shape=jax.ShapeDtypeStruct((n, d), out_dtype),
        compiler_params=_params(("parallel",)),
        name="rmsnorm",
    )(x, g.reshape(1, d).astype(F32))


def _mm_kernel(a_ref, w_ref, o_ref):
    o_ref[...] = jnp.dot(a_ref[...], w_ref[...],
                         preferred_element_type=F32).astype(o_ref.dtype)


def _mm_res_kernel(a_ref, w_ref, r_ref, o_ref):
    y = jnp.dot(a_ref[...], w_ref[...], preferred_element_type=F32)
    o_ref[...] = (r_ref[...] + y).astype(o_ref.dtype)


def _mm(a, w, out_dtype, tm, tn, residual=None, name="mm"):
    m, k = a.shape
    n = w.shape[1]
    in_specs = [pl.BlockSpec((tm, k), lambda i, j: (i, 0)),
                pl.BlockSpec((k, tn), lambda i, j: (0, j))]
    args = [a, w]
    body = _mm_kernel
    if residual is not None:
        in_specs.append(pl.BlockSpec((tm, tn), lambda i, j: (i, j)))
        args.append(residual)
        body = _mm_res_kernel
    return pl.pallas_call(
        body,
        grid=(m // tm, n // tn),
        in_specs=in_specs,
        out_specs=pl.BlockSpec((tm, tn), lambda i, j: (i, j)),
        out_shape=jax.ShapeDtypeStruct((m, n), out_dtype),
        compiler_params=_params(("parallel", "parallel")),
        name=name,
    )(*args)


def _merge_kernel(a_ref, b_ref, wa_ref, wb_ref, ga_ref, gb_ref, o_ref):
    ya = jnp.dot(a_ref[...], wa_ref[...], preferred_element_type=F32)
    yb = jnp.dot(b_ref[...], wb_ref[...], preferred_element_type=F32)
    o_ref[...] = (_sigmoid(ga_ref[...]) * ya + _sigmoid(gb_ref[...]) * yb).astype(o_ref.dtype)


def _merge(a, b, wa, wb, gates, ga_col, gb_col, tm, tn):
    m, ka = a.shape
    kb = b.shape[1]
    n = wa.shape[1]
    ga_blk, gb_blk = ga_col // tn, gb_col // tn
    return pl.pallas_call(
        _merge_kernel,
        grid=(m // tm, n // tn),
        in_specs=[pl.BlockSpec((tm, ka), lambda i, j: (i, 0)),
                  pl.BlockSpec((tm, kb), lambda i, j: (i, 0)),
                  pl.BlockSpec((ka, tn), lambda i, j: (0, j)),
                  pl.BlockSpec((kb, tn), lambda i, j: (0, j)),
                  pl.BlockSpec((tm, tn), lambda i, j: (i, ga_blk + j)),
                  pl.BlockSpec((tm, tn), lambda i, j: (i, gb_blk + j))],
        out_specs=pl.BlockSpec((tm, tn), lambda i, j: (i, j)),
        out_shape=jax.ShapeDtypeStruct((m, n), BF16),
        compiler_params=_params(("parallel", "parallel")),
        name="merge",
    )(a, b, wa, wb, gates, gates)


def _mlstm_kernel(q_ref, k_ref, v_ref, vt_ref, o_ref, iar_ref, far_ref, fac_ref,
                  bi_ref, bf_ref, c0_ref, n0_ref, m0_ref, gn_ref,
                  hg_ref, c_ref, n_ref, m_ref):
    @pl.when(pl.program_id(2) == 0)
    def _():
        c_ref[...] = c0_ref[...]
        n_ref[...] = n0_ref[...]
        m_ref[...] = m0_ref[...]

    L = q_ref.shape[0]
    kscale = k_ref.shape[1] ** -0.5
    q = q_ref[...]
    k = k_ref[...]
    v = v_ref[...]
    lf_r = _log_sigmoid(far_ref[...] + bf_ref[...])
    lf_c = _log_sigmoid(fac_ref[...] + bf_ref[...])
    ig_r = iar_ref[...] + bi_ref[...]
    row = lax.broadcasted_iota(I32, (L, L), 0)
    col = lax.broadcasted_iota(I32, (L, L), 1)
    tril = col <= row
    b_c = jnp.sum(jnp.where(tril, lf_r, 0.0), axis=1, keepdims=True)
    b_r = jnp.sum(jnp.where(row <= col, lf_c, 0.0), axis=0, keepdims=True)
    logw = jnp.where(tril, b_c - b_r + ig_r, -jnp.inf)
    g = b_c + m_ref[...]
    m = jnp.maximum(g, jnp.max(logw, axis=1, keepdims=True))
    w = jnp.exp(logw - m)
    inter = jnp.exp(g - m)
    qk = lax.dot_general(q, k, (((1,), (1,)), ((), ())), preferred_element_type=F32)
    s = w * qk * kscale
    c0 = c_ref[...]
    n0 = n_ref[...]
    qc = lax.dot_general(q, c0.astype(BF16), (((1,), (1,)), ((), ())),
                         preferred_element_type=F32)
    num = jnp.dot(s.astype(BF16), v, preferred_element_type=F32) + inter * qc
    den = (jnp.sum(s, axis=1, keepdims=True)
           + inter * jnp.sum(q.astype(F32) * n0, axis=1, keepdims=True))
    h = num / jnp.maximum(jnp.abs(den), jnp.exp(-m))
    hn = h * lax.rsqrt(jnp.mean(h * h, axis=1, keepdims=True) + EPS) * gn_ref[...]
    hg_ref[...] = (_sigmoid(o_ref[...]) * hn).astype(hg_ref.dtype)
    w_last = w[L - 1:L, :]
    decay = inter[L - 1:L, :]
    wvt = (vt_ref[...].astype(F32) * w_last).astype(BF16)
    c_ref[...] = decay * c0 + jnp.dot(wvt, k, preferred_element_type=F32) * kscale
    wk = jnp.dot(jnp.broadcast_to(w_last, (8, L)).astype(BF16), k,
                 preferred_element_type=F32)
    n_ref[...] = decay * n0 + wk[0:1, :] * kscale
    m_ref[...] = m[L - 1:L, :]


def _mlstm(pb, pf, vt, ia_r, fa_r, fa_c, b_i, b_f, c0, n0, m0, gn, *,
           batch, seq, blk, row0, q_col, k_col, v_col, o_col):
    nc = seq // blk
    rb0 = row0 // blk

    def rows(b, h, c):
        return rb0 + b * nc + c

    qb, kb, vb, ob = q_col // DK_A, k_col // DK_A, v_col // DV_A, o_col // DV_A
    st = lambda b, h, c: (b, h, 0, 0)
    out_shape = (jax.ShapeDtypeStruct((batch * seq, NH_A * DV_A), BF16),
                 jax.ShapeDtypeStruct((batch, NH_A, DV_A, DK_A), F32),
                 jax.ShapeDtypeStruct((batch, NH_A, 1, DK_A), F32),
                 jax.ShapeDtypeStruct((batch, NH_A, 1, 1), F32))
    return pl.pallas_call(
        _mlstm_kernel,
        grid=(batch, NH_A, nc),
        in_specs=[
            pl.BlockSpec((blk, DK_A), lambda b, h, c: (rows(b, h, c), qb + h)),
            pl.BlockSpec((blk, DK_A), lambda b, h, c: (rows(b, h, c), kb + h)),
            pl.BlockSpec((blk, DV_A), lambda b, h, c: (rows(b, h, c), vb + h)),
            pl.BlockSpec((None, DV_A, blk), lambda b, h, c: (b, h, c)),
            pl.BlockSpec((blk, DV_A), lambda b, h, c: (rows(b, h, c), ob + h)),
            pl.BlockSpec((None, None, 1, blk), lambda b, h, c: (b, h, 0, c)),
            pl.BlockSpec((None, None, 1, blk), lambda b, h, c: (b, h, 0, c)),
            pl.BlockSpec((None, None, blk, 1), lambda b, h, c: (b, h, c, 0)),
            pl.BlockSpec((None, 1, 1), lambda b, h, c: (h, 0, 0)),
            pl.BlockSpec((None, 1, 1), lambda b, h, c: (h, 0, 0)),
            pl.BlockSpec((None, None, DV_A, DK_A), st),
            pl.BlockSpec((None, None, 1, DK_A), st),
            pl.BlockSpec((None, None, 1, 1), st),
            pl.BlockSpec((None, 1, DV_A), lambda b, h, c: (h, 0, 0)),
        ],
        out_specs=(
            pl.BlockSpec((blk, DV_A), lambda b, h, c: (b * nc + c, h)),
            pl.BlockSpec((None, None, DV_A, DK_A), st),
            pl.BlockSpec((None, None, 1, DK_A), st),
            pl.BlockSpec((None, None, 1, 1), st),
        ),
        out_shape=out_shape,
        compiler_params=_params(("parallel", "parallel", "arbitrary")),
        name="mlstm",
    )(pb, pb, pb, vt, pf, ia_r, fa_r, fa_c, b_i, b_f, c0, n0, m0, gn)


def _kend_max(q_first, tq, s_len):
    return jnp.minimum(s_len, ((q_first + tq - 1) // CHUNK + 1) * CHUNK)


def _index_kernel(qi_ref, wi_ref, kit_ref, mask_ref, key_ref, *, tq, tk, s_len, t_len, topk):
    q_first = (s_len - t_len) + pl.program_id(1) * tq
    s_pad = mask_ref.shape[1]
    nblk = (_kend_max(q_first, tq, s_len) + tk - 1) // tk
    qpos = q_first + lax.broadcasted_iota(I32, (tq, 1), 0)
    kend = jnp.minimum(s_len, (qpos // CHUNK + 1) * CHUNK)
    wi = wi_ref[...]
    wcols = [wi[:, h:h + 1] for h in range(NI_H)]

    def score_body(jb, carry):
        k0 = pl.multiple_of(jb * tk, tk)
        kt = kit_ref[:, pl.ds(k0, tk)]
        acc = jnp.zeros((tq, tk), F32)
        for h in range(NI_H):
            d = jnp.dot(qi_ref[h], kt, preferred_element_type=F32)
            acc = acc + wcols[h] * jnp.maximum(d, 0.0)
        bits = lax.bitcast_convert_type(acc, I32)
        key = jnp.where(bits >= 0, bits, bits ^ jnp.int32(0x7FFFFFFF))
        kpos = k0 + lax.broadcasted_iota(I32, (tq, tk), 1)
        key_ref[:, pl.ds(k0, tk)] = jnp.where(kpos < kend, key, jnp.int32(INT_MIN))
        return carry

    lax.fori_loop(0, nblk, score_body, 0)

    def count_ge(cand):
        def body(jb, acc):
            k0 = pl.multiple_of(jb * tk, tk)
            ge = jnp.where(key_ref[:, pl.ds(k0, tk)] >= cand, 1.0, 0.0)
            for c in range(tk // LANE):
                acc = acc + ge[:, c * LANE:(c + 1) * LANE]
            return acc
        acc = lax.fori_loop(0, nblk, body, jnp.zeros((tq, LANE), F32))
        return jnp.sum(acc, axis=1, keepdims=True)

    def bit_body(i, t_u):
        cand_u = t_u | lax.shift_left(jnp.int32(1), 31 - i)
        cnt = count_ge(cand_u ^ jnp.int32(INT_MIN))
        return jnp.where(cnt >= float(topk), cand_u, t_u)

    t_u = lax.fori_loop(0, 32, bit_body, jnp.zeros((tq, 1), I32))
    thr = jnp.maximum(t_u ^ jnp.int32(INT_MIN), jnp.int32(INT_MIN + 1))

    def mask_body(jb, carry):
        k0 = pl.multiple_of(jb * tk, tk)
        sel = key_ref[:, pl.ds(k0, tk)] >= thr
        mask_ref[:, pl.ds(k0, tk)] = jnp.where(sel, 0.0, NEG_BIG).astype(mask_ref.dtype)
        return carry

    lax.fori_loop(0, nblk, mask_body, 0)

    def fill_body(jb, carry):
        k0 = pl.multiple_of(jb * tk, tk)
        mask_ref[:, pl.ds(k0, tk)] = jnp.full((tq, tk), NEG_BIG, mask_ref.dtype)
        return carry

    lax.fori_loop(nblk, s_pad // tk, fill_body, 0)


def _index_mask(qi_hm, wi, kit, *, batch, t_len, s_len, tq, tk, topk):
    s_pad = kit.shape[2]
    nq = t_len // tq
    kern = functools.partial(_index_kernel, tq=tq, tk=tk, s_len=s_len, t_len=t_len, topk=topk)
    return pl.pallas_call(
        kern,
        grid=(batch, nq),
        in_specs=[pl.BlockSpec((NI_H, tq, NI_D), lambda b, j: (0, b * nq + j, 0)),
                  pl.BlockSpec((tq, NI_H), lambda b, j: (b * nq + j, 0)),
                  pl.BlockSpec((None, NI_D, s_pad), lambda b, j: (b, 0, 0))],
        out_specs=pl.BlockSpec((None, tq, s_pad), lambda b, j: (b, j, 0)),
        out_shape=jax.ShapeDtypeStruct((batch, t_len, s_pad), BF16),
        scratch_shapes=[pltpu.VMEM((tq, s_pad), I32)],
        compiler_params=_params(("parallel", "parallel")),
        name="dsa_index",
    )(qi_hm, wi, kit)


def _t5_bucket(d):
    n = jnp.abs(d)
    large = jnp.full(d.shape, 8, I32)
    for start in T5_LOG_STARTS:
        large = large + (n >= start).astype(I32)
    return jnp.where(d > 0, N_BUCKETS // 2, 0) + jnp.where(n < 8, n, large)


def _relbias_table_kernel(rb_ref, o_ref, *, tq):
    off = -LANE * pl.program_id(0)
    h = pl.program_id(1)
    d = (off + lax.broadcasted_iota(I32, (tq, LANE), 1)
         - lax.broadcasted_iota(I32, (tq, LANE), 0))
    bucket = _t5_bucket(d)
    x = jnp.zeros((tq, LANE), F32)
    for b in range(N_BUCKETS):
        x = jnp.where(bucket == b, rb_ref[b, h], x)
    o_ref[...] = x


def _relbias_table(rel_bias, tq):
    return pl.pallas_call(
        functools.partial(_relbias_table_kernel, tq=tq),
        grid=(2, NH_B),
        in_specs=[pl.BlockSpec(memory_space=pltpu.SMEM)],
        out_specs=pl.BlockSpec((None, None, tq, LANE), lambda o, h: (o, h, 0, 0)),
        out_shape=jax.ShapeDtypeStruct((2, NH_B, tq, LANE), F32),
        name="relbias_table",
    )(rel_bias.astype(F32))


def _attn_kernel(rb_ref, q_ref, k_ref, v_ref, mask_ref, tab_ref, o_ref,
                 m_sc, l_sc, acc_sc, rb_sc, *, tq, tk, s_len, t_len):
    jk = pl.program_id(2)
    q_first = (s_len - t_len) + pl.program_id(1) * tq
    nblk = (_kend_max(q_first, tq, s_len) + tk - 1) // tk
    group = NH_B // KVH_B
    scale = HD_B ** -0.5

    @pl.when(jk == 0)
    def _():
        m_sc[...] = jnp.full(m_sc.shape, NEG_BIG, F32)
        l_sc[...] = jnp.zeros(l_sc.shape, F32)
        acc_sc[...] = jnp.zeros(acc_sc.shape, F32)

    def process(bias_of_head):
        maskf = mask_ref[...].astype(F32)
        for g in range(KVH_B):
            kg = k_ref[:, g * HD_B:(g + 1) * HD_B]
            vg = v_ref[:, g * HD_B:(g + 1) * HD_B]
            heads = range(g * group, (g + 1) * group)
            qg = jnp.concatenate([q_ref[:, h * HD_B:(h + 1) * HD_B] for h in heads], axis=0)
            s = lax.dot_general(qg, kg, (((1,), (1,)), ((), ())),
                                preferred_element_type=F32)
            bias = jnp.concatenate([maskf + bias_of_head(h) for h in heads], axis=0)
            x = s * scale + bias
            rows = pl.ds(g * group * tq, group * tq)
            m_old = m_sc[rows, :]
            m_new = jnp.maximum(m_old, jnp.max(x, axis=1, keepdims=True))
            p = jnp.exp(x - m_new)
            alpha = jnp.exp(m_old - m_new)
            l_sc[rows, :] = alpha * l_sc[rows, :] + jnp.sum(p, axis=1, keepdims=True)
            acc_sc[rows, :] = alpha * acc_sc[rows, :] + jnp.dot(
                p.astype(BF16), vg, preferred_element_type=F32)
            m_sc[rows, :] = m_new

    k0 = jk * tk
    c_near0 = q_first - k0
    c_near1 = q_first - LANE - k0
    is_near = jnp.logical_and(jk < nblk, k0 + tk - 1 - q_first > -T5_FAR)

    @pl.when(jnp.logical_and(jk < nblk, jnp.logical_not(is_near)))
    def _():
        process(lambda h: rb_ref[N_BUCKETS // 2 - 1, h])

    @pl.when(is_near)
    def _():
        for h in range(NH_B):
            rb_sc[h] = jnp.full((tq, tk), rb_ref[N_BUCKETS // 2 - 1, h], F32)

        @pl.when(jnp.logical_and(c_near0 >= 0, c_near0 < tk))
        def _():
            off = pl.multiple_of(c_near0, LANE)
            for h in range(NH_B):
                rb_sc[h, :, pl.ds(off, LANE)] = tab_ref[0, h]

        @pl.when(jnp.logical_and(c_near1 >= 0, c_near1 < tk))
        def _():
            off = pl.multiple_of(c_near1, LANE)
            for h in range(NH_B):
                rb_sc[h, :, pl.ds(off, LANE)] = tab_ref[1, h]

        process(lambda h: rb_sc[h])

    @pl.when(jk == pl.num_programs(2) - 1)
    def _():
        out = acc_sc[...] / l_sc[...]
        for h in range(NH_B):
            o_ref[:, h * HD_B:(h + 1) * HD_B] = out[h * tq:(h + 1) * tq, :].astype(o_ref.dtype)


def _attention(rel_bias, tab, pb, kall, vall, mask, *, batch, t_len, s_len, tq, tk, row0, q_col):
    s_pad = kall.shape[1]
    nq, nk = t_len // tq, s_pad // tk
    qw = NH_B * HD_B
    rq0, cq = row0 // tq, q_col // qw

    def kblk(b, j, kk):
        q_first = (s_len - t_len) + j * tq
        nblk = (_kend_max(q_first, tq, s_len) + tk - 1) // tk
        return jnp.minimum(kk, nblk - 1)

    kern = functools.partial(_attn_kernel, tq=tq, tk=tk, s_len=s_len, t_len=t_len)
    return pl.pallas_call(
        kern,
        grid=(batch, nq, nk),
        in_specs=[pl.BlockSpec(memory_space=pltpu.SMEM),
                  pl.BlockSpec((tq, qw), lambda b, j, kk: (rq0 + b * nq + j, cq)),
                  pl.BlockSpec((None, tk, KVH_B * HD_B), lambda b, j, kk: (b, kblk(b, j, kk), 0)),
                  pl.BlockSpec((None, tk, KVH_B * HD_B), lambda b, j, kk: (b, kblk(b, j, kk), 0)),
                  pl.BlockSpec((None, tq, tk), lambda b, j, kk: (b, j, kblk(b, j, kk))),
                  pl.BlockSpec((2, NH_B, tq, LANE), lambda b, j, kk: (0, 0, 0, 0))],
        out_specs=pl.BlockSpec((tq, qw), lambda b, j, kk: (b * nq + j, 0)),
        out_shape=jax.ShapeDtypeStruct((batch * t_len, qw), BF16),
        scratch_shapes=[pltpu.VMEM((NH_B * tq, 1), F32),
                        pltpu.VMEM((NH_B * tq, 1), F32),
                        pltpu.VMEM((NH_B * tq, HD_B), F32),
                        pltpu.VMEM((NH_B, tq, tk), F32)],
        compiler_params=_params(("parallel", "parallel", "arbitrary")),
        name="dsa_attn",
    )(rel_bias.astype(F32), pb, kall, vall, mask, tab)


def _router_kernel(x_ref, g_ref, w_ref, b_ref, h_ref, r_ref):
    x = x_ref[...]
    ms = jnp.mean(x * x, axis=-1, keepdims=True)
    h = x * lax.rsqrt(ms + EPS) * g_ref[...]
    h_ref[...] = h
    logits = jnp.dot(h, w_ref[...], preferred_element_type=F32,
                     precision=lax.Precision.HIGHEST) + b_ref[...]
    lane = lax.broadcasted_iota(I32, logits.shape, 1)
    big = jnp.int32(LANE)
    is_grp = jnp.logical_and(lane >= N_EXPERTS, lane < N_EXPERTS + N_GROUPS)
    gl = jnp.where(is_grp, logits, -jnp.inf)
    gmax = jnp.max(gl, axis=1, keepdims=True)
    p_grp = 1.0 / jnp.sum(jnp.exp(gl - gmax), axis=1, keepdims=True)
    grp = jnp.min(jnp.where(gl == gmax, lane, big), axis=1, keepdims=True) - N_EXPERTS
    in_grp = (lane // EXPERTS_PER_GROUP) == grp
    el = jnp.where(in_grp, logits, -jnp.inf)
    e1 = jnp.max(el, axis=1, keepdims=True)
    i1 = jnp.min(jnp.where(el == e1, lane, big), axis=1, keepdims=True)
    el2 = jnp.where(lane == i1, -jnp.inf, el)
    e2 = jnp.max(el2, axis=1, keepdims=True)
    i2 = jnp.min(jnp.where(el2 == e2, lane, big), axis=1, keepdims=True)
    z = jnp.exp(e2 - e1)
    p1 = p_grp / (1.0 + z)
    p2 = p_grp * z / (1.0 + z)
    r = jnp.where(lane == 0, i1.astype(F32), 0.0)
    r = jnp.where(lane == 1, i2.astype(F32), r)
    r = jnp.where(lane == 2, p1, r)
    r = jnp.where(lane == 3, p2, r)
    r_ref[...] = r


def _router(x, g, w_r, b_r, tm=256):
    n, d = x.shape
    return pl.pallas_call(
        _router_kernel,
        grid=(n // tm,),
        in_specs=[pl.BlockSpec((tm, d), lambda i: (i, 0)),
                  pl.BlockSpec((1, d), lambda i: (0, 0)),
                  pl.BlockSpec((d, LANE), lambda i: (0, 0)),
                  pl.BlockSpec((1, LANE), lambda i: (0, 0))],
        out_specs=(pl.BlockSpec((tm, d), lambda i: (i, 0)),
                   pl.BlockSpec((tm, LANE), lambda i: (i, 0))),
        out_shape=(jax.ShapeDtypeStruct((n, d), F32),
                   jax.ShapeDtypeStruct((n, LANE), F32)),
        compiler_params=_params(("parallel",)),
        name="router",
    )(x, g.reshape(1, d).astype(F32), w_r, b_r)


def _gather_rows_kernel(idx_ref, src_ref, o_ref, sem, *, tr):
    base = pl.program_id(0) * tr

    def row_copy(r):
        return pltpu.make_async_copy(src_ref.at[pl.ds(idx_ref[base + r], 1)],
                                     o_ref.at[pl.ds(r, 1)], sem)

    def start(r, carry):
        row_copy(r).start()
        return carry

    def wait(r, carry):
        row_copy(r).wait()
        return carry

    lax.fori_loop(0, tr, start, 0)
    lax.fori_loop(0, tr, wait, 0)


def _gather_rows(src, idx, tr):
    n_out = idx.shape[0]
    d = src.shape[1]
    return pl.pallas_call(
        functools.partial(_gather_rows_kernel, tr=tr),
        grid_spec=pltpu.PrefetchScalarGridSpec(
            num_scalar_prefetch=1,
            grid=(n_out // tr,),
            in_specs=[pl.BlockSpec(memory_space=pl.ANY)],
            out_specs=pl.BlockSpec((tr, d), lambda i, idx: (i, 0)),
            scratch_shapes=[pltpu.SemaphoreType.DMA(())]),
        out_shape=jax.ShapeDtypeStruct((n_out, d), src.dtype),
        compiler_params=_params(("arbitrary",)),
        name="moe_gather",
    )(idx, src)


def _expert_up_kernel(te_ref, nv_ref, x_ref, wg_ref, wu_ref, o_ref):
    @pl.when(pl.program_id(1) < nv_ref[0])
    def _():
        x = x_ref[...].astype(BF16)
        a = jnp.dot(x, wg_ref[...], preferred_element_type=F32)
        u = jnp.dot(x, wu_ref[...], preferred_element_type=F32)
        o_ref[...] = (a * _sigmoid(a) * u).astype(o_ref.dtype)

    @pl.when(pl.program_id(1) >= nv_ref[0])
    def _():
        o_ref[...] = jnp.zeros(o_ref.shape, o_ref.dtype)


def _expert_up(xs, wg, wu, tile_expert, n_valid, tr, tf):
    r, d = xs.shape
    f = wg.shape[2]
    return pl.pallas_call(
        _expert_up_kernel,
        grid_spec=pltpu.PrefetchScalarGridSpec(
            num_scalar_prefetch=2,
            grid=(f // tf, r // tr),
            in_specs=[pl.BlockSpec((tr, d), lambda fi, t, te, nv: (t, 0)),
                      pl.BlockSpec((None, d, tf), lambda fi, t, te, nv: (te[t], 0, fi)),
                      pl.BlockSpec((None, d, tf), lambda fi, t, te, nv: (te[t], 0, fi))],
            out_specs=pl.BlockSpec((tr, tf), lambda fi, t, te, nv: (t, fi))),
        out_shape=jax.ShapeDtypeStruct((r, f), BF16),
        compiler_params=_params(("arbitrary", "arbitrary")),
        name="moe_up",
    )(tile_expert, n_valid, xs, wg, wu)


def _expert_down_kernel(te_ref, nv_ref, a_ref, wd_ref, o_ref):
    @pl.when(pl.program_id(0) < nv_ref[0])
    def _():
        o_ref[...] = jnp.dot(a_ref[...], wd_ref[...], preferred_element_type=F32)

    @pl.when(pl.program_id(0) >= nv_ref[0])
    def _():
        o_ref[...] = jnp.zeros(o_ref.shape, o_ref.dtype)


def _expert_down(act, wd, tile_expert, n_valid, tr):
    r, f = act.shape
    d = wd.shape[2]
    return pl.pallas_call(
        _expert_down_kernel,
        grid_spec=pltpu.PrefetchScalarGridSpec(
            num_scalar_prefetch=2,
            grid=(r // tr,),
            in_specs=[pl.BlockSpec((tr, f), lambda t, te, nv: (t, 0)),
                      pl.BlockSpec((None, f, d), lambda t, te, nv: (te[t], 0, 0))],
            out_specs=pl.BlockSpec((tr, d), lambda t, te, nv: (t, 0))),
        out_shape=jax.ShapeDtypeStruct((r, d), F32),
        compiler_params=_params(("arbitrary",)),
        name="moe_down",
    )(tile_expert, n_valid, act, wd)


def _combine_kernel(pos_ref, x_ref, p_ref, g_ref, y_hbm, o_ref, buf0, buf1, sem, *, tm):
    base = pl.program_id(0) * tm

    def copies(r):
        return (pltpu.make_async_copy(y_hbm.at[pl.ds(pos_ref[2 * (base + r)], 1)],
                                      buf0.at[pl.ds(r, 1)], sem.at[0]),
                pltpu.make_async_copy(y_hbm.at[pl.ds(pos_ref[2 * (base + r) + 1], 1)],
                                      buf1.at[pl.ds(r, 1)], sem.at[1]))

    def start(r, carry):
        for cp in copies(r):
            cp.start()
        return carry

    def wait(r, carry):
        for cp in copies(r):
            cp.wait()
        return carry

    lax.fori_loop(0, tm, start, 0)
    lax.fori_loop(0, tm, wait, 0)
    p = p_ref[...]
    x = x_ref[...] + (p[:, 2:3] * buf0[...] + p[:, 3:4] * buf1[...])
    ms = jnp.mean(x * x, axis=-1, keepdims=True)
    o_ref[...] = x * lax.rsqrt(ms + EPS) * g_ref[...]


def _combine(x, route, y_sorted, pos, g_final, tm=256):
    n, d = x.shape
    return pl.pallas_call(
        functools.partial(_combine_kernel, tm=tm),
        grid_spec=pltpu.PrefetchScalarGridSpec(
            num_scalar_prefetch=1,
            grid=(n // tm,),
            in_specs=[pl.BlockSpec((tm, d), lambda i, pos: (i, 0)),
                      pl.BlockSpec((tm, LANE), lambda i, pos: (i, 0)),
                      pl.BlockSpec((1, d), lambda i, pos: (0, 0)),
                      pl.BlockSpec(memory_space=pl.ANY)],
            out_specs=pl.BlockSpec((tm, d), lambda i, pos: (i, 0)),
            scratch_shapes=[pltpu.VMEM((tm, d), F32),
                            pltpu.VMEM((tm, d), F32),
                            pltpu.SemaphoreType.DMA((2,))]),
        out_shape=jax.ShapeDtypeStruct((n, d), F32),
        compiler_params=_params(("arbitrary",)),
        name="moe_combine",
    )(pos, x, route, g_final.reshape(1, d).astype(F32), y_sorted)


def _moe_layout(eid, tr):
    n = eid.shape[0]
    flat_e = eid.reshape(-1)
    n_rows = 2 * n + N_EXPERTS * tr
    onehot = (flat_e[:, None] == jnp.arange(N_EXPERTS, dtype=I32)[None, :]).astype(I32)
    rank = jnp.take_along_axis(jnp.cumsum(onehot, axis=0), flat_e[:, None], axis=1)[:, 0] - 1
    counts = jnp.sum(onehot, axis=0)
    padded = ((counts + tr - 1) // tr) * tr
    ends = jnp.cumsum(padded)
    pos = (ends - padded)[flat_e] + rank
    src = jnp.zeros((n_rows,), I32).at[pos].set(jnp.arange(2 * n, dtype=I32) // 2)
    n_valid = (ends[-1] // tr).astype(I32)
    tile_start = jnp.arange(n_rows // tr, dtype=I32) * tr
    tile_expert = jnp.searchsorted(ends, tile_start, side="right").astype(I32)
    last_expert = tile_expert[jnp.maximum(n_valid - 1, 0)]
    tile_expert = jnp.where(jnp.arange(n_rows // tr) < n_valid, tile_expert, last_expert)
    return src, pos.astype(I32), tile_expert, n_valid.reshape(1)


PB_Q, PB_K, PB_V = 0, NH_A * DK_A, 2 * NH_A * DK_A
PB_QB = PB_V + NH_A * DV_A
PB_QI = PB_QB + NH_B * HD_B
PB_COLS = PB_QI + NI_H * NI_D
PF_KB, PF_VB = 0, KVH_B * HD_B
PF_O = 2 * KVH_B * HD_B
PF_GA = PF_O + NH_A * DV_A
PF_GB = PF_GA + D_MODEL
PF_SMALL = PF_GB + D_MODEL
PF_SMALL_W = 512
PF_COLS = PF_SMALL + PF_SMALL_W


def _split_w_in(w_in):
    sizes = (NH_A * DK_A, NH_A * DK_A, NH_A * DV_A, NH_A * DV_A, NH_A, NH_A,
             NH_B * HD_B, KVH_B * HD_B, KVH_B * HD_B, NI_H * NI_D, NI_D, NI_H, D_MODEL, D_MODEL)
    parts, c = [], 0
    for s in sizes:
        parts.append(w_in[:, c:c + s])
        c += s
    qa, ka, va, oa, ia, fa, qb, kb, vb, qi, ki, wi, ga, gb = parts
    small = jnp.concatenate([ia, fa, ki, wi], axis=1)
    small = jnp.pad(small, ((0, 0), (0, PF_SMALL_W - small.shape[1])))
    w_b = jnp.concatenate([qa, ka, va, qb, qi], axis=1).astype(BF16)
    w_f = jnp.concatenate([kb, vb, oa, ga, gb, small], axis=1).astype(BF16)
    return w_b, w_f


def _dsa(rel_bias, pb, qi_hm, wi, kall, vall, kit, *, batch, t_len, s_len, tq, tk, row0):
    topk = min(TOPK_MAX, s_len // 4)
    mask = _index_mask(qi_hm, wi, kit, batch=batch, t_len=t_len, s_len=s_len,
                       tq=tq, tk=tk, topk=topk)
    tab = _relbias_table(rel_bias, tq)
    return _attention(rel_bias, tab, pb, kall, vall, mask, batch=batch, t_len=t_len,
                      s_len=s_len, tq=tq, tk=tk, row0=row0, q_col=PB_QB)


def kernel(x_prompt, x_sample, cache_k, cache_v, cache_kidx, state_C, state_n, state_m, g_mix, w_in, b_igate, b_fgate, g_mhnorm, w_a_up, w_b_up, w_out, rel_bias, g_ffn, w_rgrp, b_rgrp, w_rexp, b_rexp, w_e_gate, w_e_up, w_e_down, g_final):
    bp, tp, d = x_prompt.shape
    bs, ts, _ = x_sample.shape
    past = cache_k.shape[2]
    n_p, n_s = bp * tp, bs * ts
    n = n_p + n_s
    x = jnp.concatenate([x_prompt.reshape(n_p, d), x_sample.reshape(n_s, d)], axis=0)

    h = _rmsnorm(x, g_mix[0], BF16)
    w_b, w_f = _split_w_in(w_in[0])
    pb = _mm(h, w_b, BF16, 1024, 512, name="proj_bf16")
    pf = _mm(h, w_f, F32, 1024, 512, name="proj_f32")
    small = pf[:, PF_SMALL:PF_SMALL + 2 * NH_A + NI_D + NI_H]
    ia, fa = small[:, :NH_A], small[:, NH_A:2 * NH_A]
    ki = small[:, 2 * NH_A:2 * NH_A + NI_D]
    wi = small[:, 2 * NH_A + NI_D:]
    k_new = pf[:, PF_KB:PF_KB + KVH_B * HD_B]
    v_new = pf[:, PF_VB:PF_VB + KVH_B * HD_B]

    b_i = b_igate[0].astype(F32).reshape(NH_A, 1, 1)
    b_f = b_fgate[0].astype(F32).reshape(NH_A, 1, 1)
    gn = g_mhnorm[0].astype(F32).reshape(NH_A, 1, DV_A)
    vcols = pb[:, PB_V:PB_V + NH_A * DV_A]

    def gate_rows(a, b, t):
        return a.reshape(b, t, NH_A).transpose(0, 2, 1).reshape(b, NH_A, 1, t)

    def gate_cols(a, b, t):
        return a.reshape(b, t, NH_A).transpose(0, 2, 1).reshape(b, NH_A, t, 1)

    def mlstm_path(r0, b, t, blk, c0, n0, m0):
        sl = slice(r0, r0 + b * t)
        vt = vcols[sl].reshape(b, t, NH_A * DV_A).transpose(0, 2, 1)
        return _mlstm(pb, pf, vt, gate_rows(ia[sl], b, t), gate_rows(fa[sl], b, t),
                      gate_cols(fa[sl], b, t), b_i, b_f, c0, n0, m0, gn,
                      batch=b, seq=t, blk=blk, row0=r0,
                      q_col=PB_Q, k_col=PB_K, v_col=PB_V, o_col=PF_O)

    hg_p, c_p, nn_p, m_p = mlstm_path(
        0, bp, tp, min(MLSTM_BLOCK, tp),
        jnp.zeros((bp, NH_A, DV_A, DK_A), F32), jnp.zeros((bp, NH_A, 1, DK_A), F32),
        jnp.zeros((bp, NH_A, 1, 1), F32))
    hg_s, c_s, nn_s, m_s = mlstm_path(
        n_p, bs, ts, ts,
        state_C[0].astype(F32), state_n[0].astype(F32).reshape(bs, NH_A, 1, DK_A),
        state_m[0].astype(F32).reshape(bs, NH_A, 1, 1))
    hg = jnp.concatenate([hg_p, hg_s], axis=0)

    qi_hm = pb[:, PB_QI:PB_QI + NI_H * NI_D].reshape(n, NI_H, NI_D).transpose(1, 0, 2)
    kv_bf = pf[:, :2 * KVH_B * HD_B].astype(BF16)
    kw = KVH_B * HD_B
    tk = 512
    att_p = _dsa(rel_bias, pb, qi_hm[:, :n_p], wi[:n_p],
                 kv_bf[:n_p, :kw].reshape(bp, tp, kw), kv_bf[:n_p, kw:].reshape(bp, tp, kw),
                 ki[:n_p].astype(BF16).reshape(bp, tp, NI_D).transpose(0, 2, 1),
                 batch=bp, t_len=tp, s_len=tp, tq=128, tk=tk, row0=0)
    s_len = past + ts
    s_pad = -(-s_len // tk) * tk

    def with_cache(cache, new, width):
        a = jnp.concatenate([cache.reshape(bs, past, width).astype(BF16),
                             new.astype(BF16).reshape(bs, ts, width)], axis=1)
        return jnp.pad(a, ((0, 0), (0, s_pad - s_len), (0, 0)))

    att_s = _dsa(rel_bias, pb, qi_hm[:, n_p:], wi[n_p:],
                 with_cache(cache_k[0], kv_bf[n_p:, :kw], kw),
                 with_cache(cache_v[0], kv_bf[n_p:, kw:], kw),
                 with_cache(cache_kidx[0], ki[n_p:], NI_D).transpose(0, 2, 1),
                 batch=bs, t_len=ts, s_len=s_len, tq=ts, tk=tk, row0=n_p)
    att = jnp.concatenate([att_p, att_s], axis=0)

    merged = _merge(hg, att, w_a_up[0].astype(BF16), w_b_up[0].astype(BF16), pf,
                    PF_GA, PF_GB, 512, 256)
    x1 = _mm(merged, w_out[0].astype(BF16), F32, 1024, 512, residual=x, name="out_proj")

    w_r = jnp.concatenate([w_rexp[0], w_rgrp[0]], axis=1).astype(F32)
    w_r = jnp.pad(w_r, ((0, 0), (0, LANE - w_r.shape[1])))
    b_r = jnp.concatenate([b_rexp[0], b_rgrp[0]]).astype(F32)
    b_r = jnp.pad(b_r, (0, LANE - b_r.shape[0])).reshape(1, LANE)
    h2, route = _router(x1, g_ffn[0], w_r, b_r)
    eid = route[:, 0:2].astype(I32)
    tr = 256
    src, pos, tile_expert, n_valid = _moe_layout(eid, tr)
    xs = _gather_rows(h2, src, tr)
    act = _expert_up(xs, w_e_gate[0].astype(BF16), w_e_up[0].astype(BF16),
                     tile_expert, n_valid, tr, min(512, D_EXPERT))
    y_sorted = _expert_down(act, w_e_down[0].astype(BF16), tile_expert, n_valid, tr)
    y = _combine(x1, route, y_sorted, pos, g_final)

    def states(c, nn, m, b):
        return c[None], nn.reshape(1, b, NH_A, DK_A), m.reshape(1, b, NH_A)

    def rows(a, r0, b, t, shape):
        return a[r0:r0 + b * t].reshape((1, b, t) + shape)

    return (y[:n_p].reshape(bp, tp, d), y[n_p:].reshape(bs, ts, d),
            rows(k_new, 0, bp, tp, (KVH_B, HD_B)), rows(v_new, 0, bp, tp, (KVH_B, HD_B)),
            rows(ki, 0, bp, tp, (NI_D,)),
            *states(c_p, nn_p, m_p, bp),
            rows(k_new, n_p, bs, ts, (KVH_B, HD_B)), rows(v_new, n_p, bs, ts, (KVH_B, HD_B)),
            rows(ki, n_p, bs, ts, (NI_D,)),
            *states(c_s, nn_s, m_s, bs))
```
